```python
import jax, jax.numpy as jnp
from jax import lax
import numpy as np

D_MODEL = 1024
BATCH = 8
SEQ = 2048
DEPTH = 2

CHUNK = 128
PLE_DIM = 256
A_HEADS = 8
A_HEAD_DIM = 64
A_WIDTH = A_HEADS * A_HEAD_DIM
B_GROUPS = 4
B_GROUP_DIM = 128
B_WIDTH = B_GROUPS * B_GROUP_DIM
C_GROUPS = 8
C_GROUP_DIM = 64
C_WIDTH = C_GROUPS * C_GROUP_DIM
CONV_WIDTH = 31
N_BRANCHES = 3
LN_EPS = 1e-5
DEEPNORM_ALPHA = (2 * DEPTH) ** 0.25
DEEPNORM_BETA = (8 * DEPTH) ** -0.25
IN_SIZES = (A_WIDTH, A_WIDTH, A_WIDTH, B_WIDTH, B_WIDTH, C_WIDTH, C_WIDTH, C_WIDTH,
            N_BRANCHES * D_MODEL, D_MODEL)
IN_WIDTH = sum(IN_SIZES)

kernel_name = "hybrid_gmlp_fnet_conformer_deepnorm"


def _standardize(x):
    xf = x.astype(jnp.float32)
    mu = jnp.mean(xf, axis=-1, keepdims=True)
    var = jnp.mean(jnp.square(xf - mu), axis=-1, keepdims=True)
    return ((xf - mu) * lax.rsqrt(var + LN_EPS)).astype(x.dtype)


def _layer_norm(x, g, b):
    return _standardize(x) * g + b


def _spatial_gating(u, v, ln_g, ln_b, w_s, b_s):
    u = jax.nn.gelu(u)
    v = _layer_norm(jax.nn.gelu(v), ln_g, ln_b)
    bsz, seq, _ = v.shape
    n_chunks = seq // CHUNK
    vh = v.reshape(bsz, n_chunks, CHUNK, A_HEADS, A_HEAD_DIM)
    mixed = jnp.einsum('hqk,bnkhd->bnqhd', w_s, vh) + jnp.transpose(b_s)[:, :, None]
    return u * mixed.reshape(bsz, seq, A_WIDTH)


def _fourier_mix(z):
    bsz, seq, _ = z.shape
    zg = z.astype(jnp.float32).reshape(bsz, seq, B_GROUPS, B_GROUP_DIM)
    f = jnp.fft.fftn(zg, axes=(1, 3), norm='ortho')
    return jnp.real(f).reshape(bsz, seq, B_WIDTH).astype(z.dtype)


def _conv_module(val, glu_gate, conv_w, conv_b, ln_g, ln_b):
    h = val * jax.nn.sigmoid(glu_gate)
    h = lax.conv_general_dilated(
        h, conv_w[:, None, :].astype(h.dtype), window_strides=(1,), padding='SAME',
        dimension_numbers=('NWC', 'WIO', 'NWC'),
        feature_group_count=C_WIDTH) + conv_b
    bsz, seq, _ = h.shape
    hg = _standardize(h.reshape(bsz, seq, C_GROUPS, C_GROUP_DIM))
    h = hg.reshape(bsz, seq, C_WIDTH) * ln_g + ln_b
    return jax.nn.silu(h)


def _layer(x, p_i, w_in, b_in, a_ln_g, a_ln_b, a_ws, a_bs, c_conv_w, c_conv_b,
           c_ln_g, c_ln_b, w_pa, w_pb, w_pc, w_out, b_out, w_ple, ln_g, ln_b):
    bsz, seq, _ = x.shape
    proj = x @ w_in + b_in
    split_points = [int(s) for s in np.cumsum(IN_SIZES)[:-1]]
    (a_u, a_v, a_g, b_z, b_g, c_val, c_glu, c_g, merge, ple_g) = jnp.split(
        proj, split_points, axis=-1)
    y_a = _spatial_gating(a_u, a_v, a_ln_g, a_ln_b, a_ws, a_bs) * jax.nn.silu(a_g)
    y_b = _fourier_mix(b_z) * jax.nn.silu(b_g)
    y_c = _conv_module(c_val, c_glu, c_conv_w, c_conv_b, c_ln_g, c_ln_b) * jax.nn.silu(c_g)
    gates = jax.nn.sigmoid(merge).reshape(bsz, seq, N_BRANCHES, D_MODEL)
    merged = (gates[:, :, 0] * (y_a @ w_pa)
              + gates[:, :, 1] * (y_b @ w_pb)
              + gates[:, :, 2] * (y_c @ w_pc))
    mix = merged @ w_out + b_out
    ple = jax.nn.sigmoid(ple_g) * (p_i @ w_ple)
    return _layer_norm(DEEPNORM_ALPHA * x + mix + ple, ln_g, ln_b)


def setup_inputs(seed: int = 0) -> dict:
    key = jax.random.key(seed)
    ks = jax.random.split(key, 24)
    f32 = jnp.float32

    def nrm(k, shape, scale):
        return jax.random.normal(k, shape, f32) * scale

    return {
        "x": nrm(ks[0], (BATCH, SEQ, D_MODEL), 1.0),
        "p": nrm(ks[1], (DEPTH, BATCH, SEQ, PLE_DIM), 1.0),
        "w_in": nrm(ks[2], (DEPTH, D_MODEL, IN_WIDTH), D_MODEL ** -0.5),
        "b_in": nrm(ks[3], (DEPTH, IN_WIDTH), 0.02),
        "a_ln_g": 1.0 + nrm(ks[4], (DEPTH, A_WIDTH), 0.02),
        "a_ln_b": nrm(ks[5], (DEPTH, A_WIDTH), 0.02),
        "a_ws": nrm(ks[6], (DEPTH, A_HEADS, CHUNK, CHUNK), CHUNK ** -0.5),
        "a_bs": 1.0 + nrm(ks[7], (DEPTH, A_HEADS, CHUNK), 0.1),
        "c_conv_w": nrm(ks[8], (DEPTH, CONV_WIDTH, C_WIDTH), CONV_WIDTH ** -0.5),
        "c_conv_b": nrm(ks[9], (DEPTH, C_WIDTH), 0.02),
        "c_ln_g": 1.0 + nrm(ks[10], (DEPTH, C_WIDTH), 0.02),
        "c_ln_b": nrm(ks[11], (DEPTH, C_WIDTH), 0.02),
        "w_pa": nrm(ks[12], (DEPTH, A_WIDTH, D_MODEL), DEEPNORM_BETA * A_WIDTH ** -0.5),
        "w_pb": nrm(ks[13], (DEPTH, B_WIDTH, D_MODEL), DEEPNORM_BETA * B_WIDTH ** -0.5),
        "w_pc": nrm(ks[14], (DEPTH, C_WIDTH, D_MODEL), DEEPNORM_BETA * C_WIDTH ** -0.5),
        "w_out": nrm(ks[15], (DEPTH, D_MODEL, D_MODEL), DEEPNORM_BETA * D_MODEL ** -0.5),
        "b_out": nrm(ks[16], (DEPTH, D_MODEL), 0.02),
        "w_ple": nrm(ks[17], (DEPTH, PLE_DIM, D_MODEL), DEEPNORM_BETA * PLE_DIM ** -0.5),
        "ln_g": 1.0 + nrm(ks[18], (DEPTH, D_MODEL), 0.02),
        "ln_b": nrm(ks[19], (DEPTH, D_MODEL), 0.02),
    }


def reference(x, p, w_in, b_in, a_ln_g, a_ln_b, a_ws, a_bs, c_conv_w, c_conv_b,
              c_ln_g, c_ln_b, w_pa, w_pb, w_pc, w_out, b_out, w_ple, ln_g, ln_b):
    for i in range(DEPTH):
        x = _layer(x, p[i], w_in[i], b_in[i], a_ln_g[i], a_ln_b[i], a_ws[i], a_bs[i],
                   c_conv_w[i], c_conv_b[i], c_ln_g[i], c_ln_b[i], w_pa[i], w_pb[i],
                   w_pc[i], w_out[i], b_out[i], w_ple[i], ln_g[i], ln_b[i])
    return x
```

```python
import functools
import math

import jax
import jax.numpy as jnp
import numpy as np
from jax import lax
from jax.experimental import pallas as pl
from jax.experimental.pallas import tpu as pltpu

LN_EPS = 1e-5
CHUNK = 128
A_HEADS = 8
B_GROUPS = 4
C_GROUPS = 8
CONV_WIDTH = 31
CONV_HALF = CONV_WIDTH // 2
CONV_PAD = 16
LANES = 128
V7X_VMEM_LIMIT_BYTES = 60000 * 1024

P_ROWS = 256
CONV_ROWS = 128
O_ROWS = 256

_GELU_C = math.sqrt(2.0 / math.pi)


def _gelu(x):
    return 0.5 * x * (1.0 + jnp.tanh(_GELU_C * (x + 0.044715 * (x * x * x))))


def _sigmoid(x):
    return jax.nn.sigmoid(x)


def _silu(x):
    return x * _sigmoid(x)


def _dot(a, b):
    return jnp.dot(a, b, preferred_element_type=jnp.float32)


@functools.lru_cache(maxsize=None)
def _dft_tables(seq, group_dim):
    def cs(n):
        idx = np.arange(n, dtype=np.int64)
        ang = (2.0 * np.pi / n) * ((idx[:, None] * idx[None, :]) % n).astype(np.float64)
        return np.cos(ang), np.sin(ang)
    cm, sm = cs(group_dim)
    cn, sn = cs(seq)
    chan = np.concatenate([cm, sm], axis=1) / math.sqrt(group_dim)
    seqt = np.concatenate([cn, -sn], axis=1) / math.sqrt(seq)
    return chan.astype(np.float32), seqt.astype(np.float32)


def _stage_p_kernel(x_ref, w_ref, b_ref, alng_ref, alnb_ref, ws_ref, bs_ref, cdft_ref,
                    cw_ref, cb_ref, clng_ref, clnb_ref,
                    ya_ref, zcs_ref, gb_ref, yc_ref, h_scr, cg_scr):
    seq = x_ref.shape[0]
    width = ya_ref.shape[1]
    n_pairs = width // LANES
    lane = lax.broadcasted_iota(jnp.int32, (1, LANES), 1)
    lo_mask = lane < (LANES // 2)

    zeros_halo = jnp.zeros((CONV_PAD, width), jnp.float32)
    h_scr[0:CONV_PAD, :] = zeros_halo
    h_scr[CONV_PAD + seq:CONV_PAD + seq + CONV_PAD, :] = zeros_halo

    def proj(xt, g):
        return _dot(xt, w_ref[:, g * width:(g + 1) * width]) + b_ref[:, g * width:(g + 1) * width]

    def tile_body(t, carry):
        r0 = pl.multiple_of(t * P_ROWS, P_ROWS)
        rows = pl.ds(r0, P_ROWS)
        xt = x_ref[rows, :]

        v = _gelu(proj(xt, 1))
        mu = jnp.mean(v, axis=-1, keepdims=True)
        d = v - mu
        var = jnp.mean(d * d, axis=-1, keepdims=True)
        vn = (d * lax.rsqrt(var + LN_EPS) * alng_ref[...] + alnb_ref[...]).astype(jnp.bfloat16)
        mixed_chunks = []
        for c in range(P_ROWS // CHUNK):
            vblk = vn[c * CHUNK:(c + 1) * CHUNK, :]
            outs = []
            for j in range(n_pairs):
                blk = vblk[:, j * LANES:(j + 1) * LANES]
                zero = jnp.zeros_like(blk)
                rhs = jnp.concatenate([jnp.where(lo_mask, blk, zero),
                                       jnp.where(lo_mask, zero, blk)], axis=0)
                outs.append(_dot(ws_ref[j], rhs))
            mixed_chunks.append(jnp.concatenate(outs, axis=1) + bs_ref[...])
        mixed = jnp.concatenate(mixed_chunks, axis=0)
        u = _gelu(proj(xt, 0))
        ya_ref[rows, :] = (u * mixed * _silu(proj(xt, 2))).astype(jnp.bfloat16)

        z = proj(xt, 3).astype(jnp.bfloat16)
        for g in range(n_pairs):
            zz = _dot(z[:, g * LANES:(g + 1) * LANES], cdft_ref[...])
            zcs_ref[rows, g * LANES:(g + 1) * LANES] = zz[:, :LANES].astype(jnp.bfloat16)
            zcs_ref[pl.ds(seq + r0, P_ROWS), g * LANES:(g + 1) * LANES] = (
                zz[:, LANES:].astype(jnp.bfloat16))
        gb_ref[rows, :] = _silu(proj(xt, 4)).astype(jnp.bfloat16)

        h_scr[pl.ds(CONV_PAD + r0, P_ROWS), :] = proj(xt, 5) * _sigmoid(proj(xt, 6))
        cg_scr[rows, :] = _silu(proj(xt, 7)).astype(jnp.bfloat16)
        return carry

    lax.fori_loop(0, seq // P_ROWS, tile_body, 0)

    def conv_body(t, carry):
        r0 = pl.multiple_of(t * CONV_ROWS, CONV_ROWS)
        win = h_scr[pl.ds(r0, CONV_ROWS + 2 * CONV_PAD), :]
        acc = jnp.zeros((CONV_ROWS, width), jnp.float32) + cb_ref[...]
        for k in range(CONV_WIDTH):
            off = CONV_PAD - CONV_HALF + k
            acc = acc + win[off:off + CONV_ROWS, :] * cw_ref[k:k + 1, :]
        half = LANES // 2
        outs = []
        for j in range(n_pairs):
            blk = acc[:, j * LANES:(j + 1) * LANES]
            s_lo = jnp.sum(jnp.where(lo_mask, blk, 0.0), axis=-1, keepdims=True)
            s_all = jnp.sum(blk, axis=-1, keepdims=True)
            mean = jnp.where(lo_mask, s_lo, s_all - s_lo) * (1.0 / half)
            dd = blk - mean
            sq = dd * dd
            q_lo = jnp.sum(jnp.where(lo_mask, sq, 0.0), axis=-1, keepdims=True)
            q_all = jnp.sum(sq, axis=-1, keepdims=True)
            var = jnp.where(lo_mask, q_lo, q_all - q_lo) * (1.0 / half)
            outs.append(dd * lax.rsqrt(var + LN_EPS))
        hn = jnp.concatenate(outs, axis=1) * clng_ref[...] + clnb_ref[...]
        rows = pl.ds(r0, CONV_ROWS)
        yc_ref[rows, :] = (_silu(hn) * cg_scr[rows, :].astype(jnp.float32)).astype(jnp.bfloat16)
        return carry

    lax.fori_loop(0, seq // CONV_ROWS, conv_body, 0)


def _const_spec(shape):
    nd = len(shape)
    return pl.BlockSpec(shape, lambda *_: (0,) * nd, pipeline_mode=pl.Buffered(1))


def _stage_p(xbf, w1, b1, alng, alnb, ws, bs, cdft, cw, cb, clng, clnb):
    bsz, seq, dm = xbf.shape
    width = alng.shape[1]
    consts = (w1, b1, alng, alnb, ws, bs, cdft, cw, cb, clng, clnb)
    act = jax.ShapeDtypeStruct((bsz, seq, width), jnp.bfloat16)
    act_spec = pl.BlockSpec((None, seq, width), lambda b: (b, 0, 0))
    return pl.pallas_call(
        _stage_p_kernel,
        grid=(bsz,),
        in_specs=[pl.BlockSpec((None, seq, dm), lambda b: (b, 0, 0))]
                 + [_const_spec(c.shape) for c in consts],
        out_specs=[act_spec,
                   pl.BlockSpec((None, 2 * seq, width), lambda b: (b, 0, 0)),
                   act_spec, act_spec],
        out_shape=[act, jax.ShapeDtypeStruct((bsz, 2 * seq, width), jnp.bfloat16), act, act],
        scratch_shapes=[pltpu.VMEM((seq + 2 * CONV_PAD, width), jnp.float32),
                        pltpu.VMEM((seq, width), jnp.bfloat16)],
        compiler_params=pltpu.CompilerParams(
            dimension_semantics=("arbitrary",), vmem_limit_bytes=V7X_VMEM_LIMIT_BYTES),
        name="stage_p",
    )(xbf, *consts)


def _stage_o_kernel(alpha, x_ref, xbf_ref, p_ref, ya_ref, yc_ref, gb_ref, zcs_ref, tab_ref,
                    w_ref, b_ref, wpa_ref, wpb_ref, wpc_ref, wout_ref, bout_ref, wple_ref,
                    lng_ref, lnb_ref, o_ref, obf_ref):
    dm = x_ref.shape[1]
    xb = xbf_ref[...]

    def gate(g):
        return _sigmoid(_dot(xb, w_ref[:, g * dm:(g + 1) * dm]) + b_ref[:, g * dm:(g + 1) * dm])

    yb = (_dot(tab_ref[...], zcs_ref[...]) * gb_ref[...].astype(jnp.float32)).astype(jnp.bfloat16)
    merged = gate(0) * _dot(ya_ref[...], wpa_ref[...])
    merged = merged + gate(1) * _dot(yb, wpb_ref[...])
    merged = merged + gate(2) * _dot(yc_ref[...], wpc_ref[...])
    mix = _dot(merged.astype(jnp.bfloat16), wout_ref[...]) + bout_ref[...]
    ple = gate(3) * _dot(p_ref[...].astype(jnp.bfloat16), wple_ref[...])
    r = alpha * x_ref[...] + mix + ple
    mu = jnp.mean(r, axis=-1, keepdims=True)
    d = r - mu
    var = jnp.mean(d * d, axis=-1, keepdims=True)
    out = d * lax.rsqrt(var + LN_EPS) * lng_ref[...] + lnb_ref[...]
    o_ref[...] = out
    obf_ref[...] = out.astype(jnp.bfloat16)


def _stage_o(alpha, xf, xbf, p_i, ya, yc, gb, zcs, tab, w2, b2, wpa, wpb, wpc, wout, bout,
             wple, lng, lnb):
    bsz, seq, dm = xf.shape
    width = ya.shape[2]
    ple_dim = p_i.shape[2]
    consts = (w2, b2, wpa, wpb, wpc, wout, bout, wple, lng, lnb)

    def tile(last):
        return pl.BlockSpec((None, O_ROWS, last), lambda b, t: (b, t, 0))

    return pl.pallas_call(
        functools.partial(_stage_o_kernel, alpha),
        grid=(bsz, seq // O_ROWS),
        in_specs=[tile(dm), tile(dm), tile(ple_dim), tile(width), tile(width), tile(width),
                  pl.BlockSpec((None, 2 * seq, width), lambda b, t: (b, 0, 0)),
                  pl.BlockSpec((O_ROWS, 2 * seq), lambda b, t: (t, 0))]
                 + [_const_spec(c.shape) for c in consts],
        out_specs=[tile(dm), tile(dm)],
        out_shape=[jax.ShapeDtypeStruct((bsz, seq, dm), jnp.float32),
                   jax.ShapeDtypeStruct((bsz, seq, dm), jnp.bfloat16)],
        compiler_params=pltpu.CompilerParams(
            dimension_semantics=("arbitrary", "arbitrary"),
            vmem_limit_bytes=V7X_VMEM_LIMIT_BYTES),
        name="stage_o",
    )(xf, xbf, p_i, ya, yc, gb, zcs, tab, *consts)


def kernel(x, p, w_in, b_in, a_ln_g, a_ln_b, a_ws, a_bs, c_conv_w, c_conv_b, c_ln_g, c_ln_b,
           w_pa, w_pb, w_pc, w_out, b_out, w_ple, ln_g, ln_b):
    depth = w_in.shape[0]
    bsz, seq, dm = x.shape
    width = a_ln_g.shape[1]
    head_dim = width // A_HEADS
    assert width == 4 * LANES and w_in.shape[2] == 8 * width + 4 * dm
    assert a_ws.shape[1:] == (A_HEADS, CHUNK, CHUNK) and 2 * head_dim == LANES
    assert width // C_GROUPS == LANES // 2 and width // B_GROUPS == LANES
    assert seq % P_ROWS == 0 and seq % O_ROWS == 0 and seq % CONV_ROWS == 0
    alpha = float((2 * depth) ** 0.25)
    bf = jnp.bfloat16
    split = 8 * width

    chan_np, seq_np = _dft_tables(seq, LANES)
    cdft = jnp.asarray(chan_np).astype(bf)
    tab = jnp.asarray(seq_np).astype(bf)

    xf, xbf = x, x.astype(bf)
    for i in range(depth):
        row = lambda a: a[i][None, :]
        ws = (a_ws[i].reshape(A_HEADS // 2, 2, CHUNK, CHUNK).transpose(0, 2, 1, 3)
              .reshape(A_HEADS // 2, CHUNK, 2 * CHUNK).astype(bf))
        bs = jnp.repeat(a_bs[i].T, head_dim, axis=1)
        ya, zcs, gb, yc = _stage_p(
            xbf, w_in[i, :, :split].astype(bf), b_in[i, :split][None, :], row(a_ln_g),
            row(a_ln_b), ws, bs, cdft, c_conv_w[i], row(c_conv_b), row(c_ln_g), row(c_ln_b))
        xf, xbf = _stage_o(
            alpha, xf, xbf, p[i], ya, yc, gb, zcs, tab,
            w_in[i, :, split:].astype(bf), b_in[i, split:][None, :],
            w_pa[i].astype(bf), w_pb[i].astype(bf), w_pc[i].astype(bf), w_out[i].astype(bf),
            row(b_out), w_ple[i].astype(bf), row(ln_g), row(ln_b))
    return xf
```
